```python
import math
import jax, jax.numpy as jnp
from jax import lax
import numpy as np

D_MODEL = 4096
BATCH = 2
SEQ = 8192
DEPTH = 2

HEAD_DIM = 128
MIX_WIDTH = D_MODEL
N_HEADS_A = MIX_WIDTH // HEAD_DIM // 2
N_HEADS_B = MIX_WIDTH // HEAD_DIM - N_HEADS_A
DILATED_PATTERNS = ((128, 1), (512, 4), (2048, 16))
BLK = 128
N_BUCKETS = 32
MAX_DISTANCE = 2048
N_HEADS_C = MIX_WIDTH // HEAD_DIM
Q_LORA = 1024
KV_LORA = 512
QK_NOPE = 128
QK_ROPE = 64
V_DIM = 128
ROPE_THETA = 10000.0
EPS = 1e-6
NEG_INF = -1e30
N_EVEN = (DEPTH + 1) // 2
N_ODD = DEPTH // 2
IN_EVEN = 3 * N_HEADS_A * HEAD_DIM + 3 * N_HEADS_B * HEAD_DIM + MIX_WIDTH + N_HEADS_B
IN_ODD = Q_LORA + KV_LORA + QK_ROPE + MIX_WIDTH

kernel_name = 'hybrid_dilated_fox_mla_adaln'


def rmsnorm(x, g):
    xf = x.astype(jnp.float32)
    y = xf * lax.rsqrt(jnp.mean(xf * xf, axis=-1, keepdims=True) + EPS)
    return (y * g.astype(jnp.float32)).astype(x.dtype)


def t5_bucket(n):
    max_exact = N_BUCKETS // 2
    nf = jnp.maximum(n, 1).astype(jnp.float32)
    large = max_exact + (jnp.log(nf / max_exact) / math.log(MAX_DISTANCE / max_exact)
                         * (N_BUCKETS - max_exact)).astype(jnp.int32)
    large = jnp.minimum(large, N_BUCKETS - 1)
    return jnp.where(n < max_exact, n, large)


def rope(x, positions):
    half = x.shape[-1] // 2
    inv_freq = ROPE_THETA ** (-jnp.arange(half, dtype=jnp.float32) / half)
    ang = positions.astype(jnp.float32)[..., None] * inv_freq
    cos = jnp.cos(ang)[:, :, None, :]
    sin = jnp.sin(ang)[:, :, None, :]
    x1 = x[..., :half].astype(jnp.float32)
    x2 = x[..., half:].astype(jnp.float32)
    return jnp.concatenate([x1 * cos - x2 * sin, x1 * sin + x2 * cos], axis=-1).astype(x.dtype)


def dilated_window_attention(q, k, v, rel_bias, window, dilation):
    B, S, H, Dh = q.shape
    steps = window // dilation
    span = dilation * BLK
    s_pad = -(-S // span) * span
    L = s_pad // dilation
    nb = L // BLK

    def by_residue(t):
        t = jnp.pad(t, ((0, 0), (0, s_pad - S), (0, 0), (0, 0)))
        t = t.reshape(B, L, dilation, H, Dh).transpose(0, 3, 2, 1, 4)
        return t.reshape(B, H, dilation, nb, BLK, Dh)

    def with_prev(t):
        prev = jnp.pad(t, ((0, 0), (0, 0), (0, 0), (1, 0), (0, 0), (0, 0)))[:, :, :, :-1]
        return jnp.concatenate([prev, t], axis=4)

    qr = by_residue(q)
    kb = with_prev(by_residue(k))
    vb = with_prev(by_residue(v))
    s = jnp.einsum('bhrnqd,bhrnkd->bhrnqk', qr, kb).astype(jnp.float32) * (HEAD_DIM ** -0.5)
    i = jnp.arange(BLK)[:, None]
    j = jnp.arange(2 * BLK)[None, :]
    step_dist = i + BLK - j
    band = (step_dist >= 0) & (step_dist <= steps)
    valid = band[None] & ((jnp.arange(nb)[:, None, None] > 0) | (j[None] >= BLK))
    bucket = t5_bucket(jnp.maximum(step_dist, 0) * dilation)
    bias = jnp.moveaxis(rel_bias[bucket].astype(jnp.float32), -1, 0)
    s = s + bias[None, :, None, None]
    s = jnp.where(valid[None, None, None], s, NEG_INF)
    m = jnp.max(s, axis=-1, keepdims=True)
    p = jnp.exp(s - m)
    denom = jnp.sum(p, axis=-1, keepdims=True)
    o = jnp.einsum('bhrnqk,bhrnkd->bhrnqd', (p / denom).astype(v.dtype), vb)
    lse = m[..., 0] + jnp.log(denom[..., 0])
    o = o.reshape(B, H, dilation, L, Dh).transpose(0, 3, 2, 1, 4).reshape(B, s_pad, H, Dh)[:, :S]
    lse = lse.reshape(B, H, dilation, L).transpose(0, 3, 2, 1).reshape(B, s_pad, H)[:, :S]
    return o, lse


def causal_block_attention(q, k, v, scale, log_forget_cum=None):
    B, S, H, Dk = q.shape
    nb = S // BLK
    q_blocks = q.reshape(B, nb, BLK, H, Dk).swapaxes(0, 1)
    k_pos = jnp.arange(S)
    if log_forget_cum is not None:
        F = log_forget_cum.transpose(0, 2, 1)
        F_blocks = F.reshape(B, H, nb, BLK).transpose(2, 0, 1, 3)
        xs = (jnp.arange(nb), q_blocks, F_blocks)
    else:
        xs = (jnp.arange(nb), q_blocks)

    def block(xs_n):
        n, q_blk = xs_n[0], xs_n[1]
        s = jnp.einsum('bqhd,bkhd->bhqk', q_blk, k).astype(jnp.float32) * scale
        if log_forget_cum is not None:
            s = s + (xs_n[2][..., :, None] - F[:, :, None, :])
        q_pos = n * BLK + jnp.arange(BLK)
        s = jnp.where(k_pos[None, :] <= q_pos[:, None], s, NEG_INF)
        p = jax.nn.softmax(s, axis=-1).astype(v.dtype)
        return jnp.einsum('bhqk,bkhd->bqhd', p, v)

    out = lax.map(block, xs)
    return out.swapaxes(0, 1).reshape(B, S, H, v.shape[-1])


def even_mixer(h, rel_bias, w_in, b_f, w_out):
    B, S, _ = h.shape
    wa = N_HEADS_A * HEAD_DIM
    wb = N_HEADS_B * HEAD_DIM
    proj = h @ w_in
    qa, ka, va, qb, kb, vb, z, f_logit = jnp.split(
        proj, list(np.cumsum([wa, wa, wa, wb, wb, wb, MIX_WIDTH])), axis=-1)
    ha = lambda t: t.reshape(B, S, N_HEADS_A, HEAD_DIM)
    hb = lambda t: t.reshape(B, S, N_HEADS_B, HEAD_DIM)
    outs, lses = [], []
    for window, dilation in DILATED_PATTERNS:
        o, l = dilated_window_attention(ha(qa), ha(ka), ha(va), rel_bias, window, dilation)
        outs.append(o)
        lses.append(l)
    mix_w = jax.nn.softmax(jnp.stack(lses, axis=0), axis=0)
    o_a = jnp.einsum('pbsh,pbshd->bshd', mix_w.astype(h.dtype), jnp.stack(outs, axis=0))
    log_f = jax.nn.log_sigmoid(f_logit.astype(jnp.float32) + b_f.astype(jnp.float32))
    F = jnp.cumsum(log_f, axis=1)
    o_b = causal_block_attention(hb(qb), hb(kb), hb(vb), HEAD_DIM ** -0.5, F)
    o = jnp.concatenate([o_a.reshape(B, S, wa), o_b.reshape(B, S, wb)], axis=-1)
    return (o * jax.nn.silu(z)) @ w_out


def odd_mixer(h, positions, w_in, g_q, g_kv, w_qb, w_kvb, w_out):
    B, S, _ = h.shape
    proj = h @ w_in
    cq, ckv, k_r, z = jnp.split(proj, list(np.cumsum([Q_LORA, KV_LORA, QK_ROPE])), axis=-1)
    q = (rmsnorm(cq, g_q) @ w_qb).reshape(B, S, N_HEADS_C, QK_NOPE + QK_ROPE)
    kv = (rmsnorm(ckv, g_kv) @ w_kvb).reshape(B, S, N_HEADS_C, QK_NOPE + V_DIM)
    q_nope, q_rope = q[..., :QK_NOPE], rope(q[..., QK_NOPE:], positions)
    k_nope, v = kv[..., :QK_NOPE], kv[..., QK_NOPE:]
    k_rope = rope(k_r[:, :, None, :], positions)
    q = jnp.concatenate([q_nope, q_rope], axis=-1)
    k = jnp.concatenate([k_nope, jnp.broadcast_to(k_rope, (B, S, N_HEADS_C, QK_ROPE))], axis=-1)
    o = causal_block_attention(q, k, v, (QK_NOPE + QK_ROPE) ** -0.5)
    return (o.reshape(B, S, N_HEADS_C * V_DIM) * jax.nn.silu(z)) @ w_out


def setup_inputs(seed: int = 0) -> dict:
    key = jax.random.key(seed)
    ks = jax.random.split(key, 18)
    nrm = lambda k, shape, s: jax.random.normal(k, shape, jnp.float32) * s
    x = nrm(ks[0], (BATCH, SEQ, D_MODEL), 1.0)
    c = nrm(ks[1], (BATCH, D_MODEL), 1.0)
    positions = (jax.random.randint(ks[2], (BATCH, 1), 0, 4096, dtype=jnp.int32)
                 + jnp.arange(SEQ, dtype=jnp.int32)[None, :])
    g_norm = 1.0 + nrm(ks[3], (DEPTH, D_MODEL), 0.02)
    w_ada = nrm(ks[4], (DEPTH, D_MODEL, 3 * D_MODEL), 0.1 * D_MODEL ** -0.5)
    b_ada = nrm(ks[5], (DEPTH, 3 * D_MODEL), 0.02) + jnp.concatenate(
        [jnp.zeros((2 * D_MODEL,), jnp.float32), jnp.ones((D_MODEL,), jnp.float32)])[None]
    rel_bias = nrm(ks[6], (N_BUCKETS, N_HEADS_A), 0.5)
    w_in_even = nrm(ks[7], (N_EVEN, D_MODEL, IN_EVEN), D_MODEL ** -0.5)
    b_forget = jnp.linspace(1.0, 5.0, N_HEADS_B, dtype=jnp.float32)[None] + nrm(ks[8], (N_EVEN, N_HEADS_B), 0.1)
    w_out_even = nrm(ks[9], (N_EVEN, MIX_WIDTH, D_MODEL), MIX_WIDTH ** -0.5)
    w_in_odd = nrm(ks[10], (N_ODD, D_MODEL, IN_ODD), D_MODEL ** -0.5)
    g_q_lora = 1.0 + nrm(ks[11], (N_ODD, Q_LORA), 0.02)
    g_kv_lora = 1.0 + nrm(ks[12], (N_ODD, KV_LORA), 0.02)
    w_q_b = nrm(ks[13], (N_ODD, Q_LORA, N_HEADS_C * (QK_NOPE + QK_ROPE)), Q_LORA ** -0.5)
    w_kv_b = nrm(ks[14], (N_ODD, KV_LORA, N_HEADS_C * (QK_NOPE + V_DIM)), KV_LORA ** -0.5)
    w_out_odd = nrm(ks[15], (N_ODD, MIX_WIDTH, D_MODEL), MIX_WIDTH ** -0.5)
    g_final = 1.0 + nrm(ks[16], (D_MODEL,), 0.02)
    return {'x': x, 'c': c, 'positions': positions, 'g_norm': g_norm, 'w_ada': w_ada,
            'b_ada': b_ada, 'rel_bias': rel_bias, 'w_in_even': w_in_even, 'b_forget': b_forget,
            'w_out_even': w_out_even, 'w_in_odd': w_in_odd, 'g_q_lora': g_q_lora,
            'g_kv_lora': g_kv_lora, 'w_q_b': w_q_b, 'w_kv_b': w_kv_b, 'w_out_odd': w_out_odd,
            'g_final': g_final}


def reference(x, c, positions, g_norm, w_ada, b_ada, rel_bias, w_in_even, b_forget, w_out_even,
              w_in_odd, g_q_lora, g_kv_lora, w_q_b, w_kv_b, w_out_odd, g_final):
    for layer in range(DEPTH):
        mod = jax.nn.silu(c) @ w_ada[layer] + b_ada[layer]
        shift, scale, gate = jnp.split(mod, 3, axis=-1)
        h = rmsnorm(x, g_norm[layer]) * (1.0 + scale[:, None, :]) + shift[:, None, :]
        i = layer // 2
        if layer % 2 == 0:
            y = even_mixer(h, rel_bias, w_in_even[i], b_forget[i], w_out_even[i])
        else:
            y = odd_mixer(h, positions, w_in_odd[i], g_q_lora[i], g_kv_lora[i],
                          w_q_b[i], w_kv_b[i], w_out_odd[i])
        x = x + gate[:, None, :] * y
    return rmsnorm(x, g_final)
```

```python
import functools
import math

import jax
import jax.numpy as jnp
from jax import lax
from jax.experimental import pallas as pl
from jax.experimental.pallas import tpu as pltpu

LANES = 128
HEAD_DIM = 128
DILATED_PATTERNS = ((128, 1), (512, 4), (2048, 16))
BLK = 128
DIL_SPAN = 2048
N_BUCKETS = 32
MAX_DISTANCE = 2048
Q_LORA = 1024
KV_LORA = 512
QK_NOPE = 128
QK_ROPE = 64
V_DIM = 128
ROPE_THETA = 10000.0
EPS = 1e-6
NEG_INF = -1e30
VMEM_LIMIT_BYTES = 56 * 1024 * 1024

F32 = jnp.float32
BF16 = jnp.bfloat16


def _params(*sem):
    return pltpu.CompilerParams(dimension_semantics=sem, vmem_limit_bytes=VMEM_LIMIT_BYTES)


def _silu(x):
    return x / (1.0 + jnp.exp(-x))


def _dot(a, b):
    return jnp.dot(a, b, preferred_element_type=F32)


def _dot_nt(a, b):
    return lax.dot_general(a, b, (((1,), (1,)), ((), ())), preferred_element_type=F32)


def _mod_kernel(c_ref, w_ref, b_ref, o_ref):
    a = _silu(c_ref[...]).astype(BF16)
    o_ref[0] = _dot(a, w_ref[0].astype(BF16)) + b_ref[0]


def _modulation(c, w_ada, b_ada):
    depth, d, n = w_ada.shape
    rows = 8
    c_pad = jnp.pad(c, ((0, rows - c.shape[0]), (0, 0)))
    tn = min(512, n)
    out = pl.pallas_call(
        _mod_kernel,
        out_shape=jax.ShapeDtypeStruct((depth, rows, n), F32),
        grid=(depth, n // tn),
        in_specs=[pl.BlockSpec((rows, d), lambda l, j: (0, 0)),
                  pl.BlockSpec((1, d, tn), lambda l, j: (l, 0, j)),
                  pl.BlockSpec((1, 1, tn), lambda l, j: (l, 0, j))],
        out_specs=pl.BlockSpec((1, rows, tn), lambda l, j: (l, 0, j)),
        compiler_params=_params("arbitrary", "arbitrary"),
        name="adaln_modulation",
    )(c_pad, w_ada, b_ada.reshape(depth, 1, n))
    return out[:, :c.shape[0], :]


def _inproj_kernel(x_ref, g_ref, sc_ref, sh_ref, w_ref, cs_ref, *rest, tm, tn, with_side, row_chunk):
    if with_side:
        ws_ref, o_ref, side_ref, h_scr = rest
    else:
        o_ref, h_scr = rest

    @pl.when(pl.program_id(1) == 0)
    def _():
        g = g_ref[...]
        mult = 1.0 + sc_ref[0]
        shift = sh_ref[0]

        def body(c, carry):
            r0 = pl.multiple_of(c * row_chunk, row_chunk)
            x = x_ref[pl.ds(r0, row_chunk), :]
            ms = jnp.mean(x * x, axis=-1, keepdims=True)
            y = x * lax.rsqrt(ms + EPS) * g
            h_scr[pl.ds(r0, row_chunk), :] = (y * mult + shift).astype(BF16)
            return carry

        lax.fori_loop(0, tm // row_chunk, body, 0)
        if with_side:
            side_ref[...] = _dot(h_scr[...], ws_ref[...])

    acc = _dot(h_scr[...], w_ref[...]) * cs_ref[...]
    for gi in range(tn // LANES):
        o_ref[gi] = acc[:, gi * LANES:(gi + 1) * LANES].astype(BF16)


def _in_projection(x2, g, scale, shift, w, col_scale, w_side, seq, name):
    m, d = x2.shape
    n = w.shape[1]
    tm = min(512, seq)
    tn = min(1024, n)
    assert m % tm == 0 and n % tn == 0 and seq % tm == 0
    per_batch = seq // tm
    with_side = w_side is not None
    in_specs = [pl.BlockSpec((tm, d), lambda i, j: (i, 0)),
                pl.BlockSpec((1, d), lambda i, j: (0, 0)),
                pl.BlockSpec((1, 1, d), lambda i, j: (i // per_batch, 0, 0)),
                pl.BlockSpec((1, 1, d), lambda i, j: (i // per_batch, 0, 0)),
                pl.BlockSpec((d, tn), lambda i, j: (0, j)),
                pl.BlockSpec((1, tn), lambda i, j: (0, j))]
    args = [x2, g.reshape(1, d), scale[:, None, :], shift[:, None, :], w, col_scale]
    out_shape = [jax.ShapeDtypeStruct((n // LANES, m, LANES), BF16)]
    out_specs = [pl.BlockSpec((tn // LANES, tm, LANES), lambda i, j: (j, i, 0))]
    if with_side:
        in_specs.append(pl.BlockSpec((d, LANES), lambda i, j: (0, 0)))
        args.append(w_side)
        out_shape.append(jax.ShapeDtypeStruct((m, LANES), F32))
        out_specs.append(pl.BlockSpec((tm, LANES), lambda i, j: (i, 0)))
    kern = functools.partial(_inproj_kernel, tm=tm, tn=tn, with_side=with_side, row_chunk=min(128, tm))
    return pl.pallas_call(
        kern, out_shape=out_shape, grid=(m // tm, n // tn),
        in_specs=in_specs, out_specs=out_specs,
        scratch_shapes=[pltpu.VMEM((tm, d), BF16)],
        compiler_params=_params("arbitrary", "arbitrary"),
        name=name,
    )(*args)


def _forget_cumsum_kernel(f_ref, b_ref, o_ref, *, blk):
    seq = f_ref.shape[0]
    row = lax.broadcasted_iota(jnp.int32, (blk, blk), 0)
    col = lax.broadcasted_iota(jnp.int32, (blk, blk), 1)
    tri = jnp.where(col <= row, 1.0, 0.0).astype(BF16)
    bias = b_ref[...]

    def body(i, carry):
        r0 = pl.multiple_of(i * blk, blk)
        x = f_ref[pl.ds(r0, blk), :] + bias
        log_f = jnp.minimum(x, 0.0) - jnp.log(1.0 + jnp.exp(-jnp.abs(x)))
        hi = log_f.astype(BF16)
        rem = log_f - hi.astype(F32)
        mid = rem.astype(BF16)
        lo = (rem - mid.astype(F32)).astype(BF16)
        cum = _dot(tri, hi) + _dot(tri, mid) + _dot(tri, lo) + carry
        o_ref[pl.ds(r0, blk), :] = cum
        return cum[blk - 1:blk, :]

    lax.fori_loop(0, seq // blk, body, jnp.zeros((1, LANES), F32))


def _forget_cumsum(f_logit, b_pad, batch, seq):
    blk = min(256, seq)
    return pl.pallas_call(
        functools.partial(_forget_cumsum_kernel, blk=blk),
        out_shape=jax.ShapeDtypeStruct(f_logit.shape, F32),
        grid=(batch,),
        in_specs=[pl.BlockSpec((seq, LANES), lambda b: (b, 0)),
                  pl.BlockSpec((1, LANES), lambda b: (0, 0))],
        out_specs=pl.BlockSpec((seq, LANES), lambda b: (b, 0)),
        compiler_params=_params("arbitrary"),
        name="forget_cumsum",
    )(f_logit, b_pad)


def _dilated_kernel(rb_ref, bucket_ref, q_ref, k_ref, v_ref, o_ref,
                    qf, kf, vf, bias_scr, o_span, l_span, *, seq):
    h = pl.program_id(1)
    qf[...] = q_ref[0].astype(F32)
    kf[...] = k_ref[0].astype(F32)
    vf[...] = v_ref[0].astype(F32)

    row = lax.broadcasted_iota(jnp.int32, (BLK, 2 * BLK), 0)
    col = lax.broadcasted_iota(jnp.int32, (BLK, 2 * BLK), 1)
    step_dist = row + BLK - col
    band = (step_dist >= 0) & (step_dist <= BLK)
    for p in range(len(DILATED_PATTERNS)):
        bucket = bucket_ref[p]
        bias = jnp.zeros((BLK, 2 * BLK), F32)
        for b in range(N_BUCKETS):
            bias = jnp.where(bucket == b, rb_ref[b, h], bias)
        bias_scr[p] = jnp.where(band, bias, NEG_INF)

    def rows(start, size, stride):
        if stride == 1:
            return pl.ds(start, size)
        return pl.ds(start, size, stride=stride)

    def span_body(sp, carry):
        span0 = sp * DIL_SPAN
        for p, (_, dil) in enumerate(DILATED_PATTERNS):
            tiles = DIL_SPAN // BLK

            def tile_body(t, c, p=p, dil=dil):
                blk_local = t // dil
                res = t % dil
                local0 = blk_local * (BLK * dil) + res
                base = span0 + local0
                prev = jnp.maximum(base - BLK * dil, 0)
                first = base < BLK * dil
                q = qf[rows(base, BLK, dil), :].astype(BF16)
                k_own = kf[rows(base, BLK, dil), :].astype(BF16)
                v_own = vf[rows(base, BLK, dil), :].astype(BF16)
                k_prev = kf[rows(prev, BLK, dil), :].astype(BF16)
                v_prev = vf[rows(prev, BLK, dil), :].astype(BF16)
                s_prev = _dot_nt(q, k_prev) + bias_scr[p, :, 0:BLK]
                s_prev = jnp.where(first, NEG_INF, s_prev)
                s_own = _dot_nt(q, k_own) + bias_scr[p, :, BLK:2 * BLK]
                m = jnp.maximum(jnp.max(s_prev, axis=-1, keepdims=True),
                                jnp.max(s_own, axis=-1, keepdims=True))
                p_prev = jnp.exp(s_prev - m)
                p_own = jnp.exp(s_own - m)
                denom = (jnp.sum(p_prev, axis=-1, keepdims=True)
                         + jnp.sum(p_own, axis=-1, keepdims=True))
                inv = 1.0 / denom
                o = (_dot((p_prev * inv).astype(BF16), v_prev)
                     + _dot((p_own * inv).astype(BF16), v_own))
                lse = m + jnp.log(denom)
                o_span[p, rows(local0, BLK, dil), :] = o
                l_span[p, rows(local0, BLK, dil), :] = jnp.broadcast_to(lse, (BLK, LANES))
                return c

            lax.fori_loop(0, tiles, tile_body, 0)

        chunk = 256

        def mix_body(ci, c):
            r0 = pl.multiple_of(ci * chunk, chunk)
            ls = [l_span[p, pl.ds(r0, chunk), :] for p in range(len(DILATED_PATTERNS))]
            mx = functools.reduce(jnp.maximum, ls)
            es = [jnp.exp(l - mx) for l in ls]
            tot = functools.reduce(lambda a, b: a + b, es)
            inv = 1.0 / tot
            acc = jnp.zeros((chunk, LANES), F32)
            for p in range(len(DILATED_PATTERNS)):
                acc = acc + (es[p] * inv) * o_span[p, pl.ds(r0, chunk), :]
            o_ref[0, pl.ds(span0 + r0, chunk), :] = acc.astype(BF16)
            return c

        lax.fori_loop(0, DIL_SPAN // chunk, mix_body, 0)
        return carry

    lax.fori_loop(0, seq // DIL_SPAN, span_body, 0)


def _dilated_buckets():
    i = jnp.arange(BLK)[:, None]
    j = jnp.arange(2 * BLK)[None, :]
    step_dist = jnp.maximum(i + BLK - j, 0)
    max_exact = N_BUCKETS // 2
    tabs = []
    for _, dil in DILATED_PATTERNS:
        n = step_dist * dil
        nf = jnp.maximum(n, 1).astype(F32)
        large = max_exact + (jnp.log(nf / max_exact) / math.log(MAX_DISTANCE / max_exact)
                             * (N_BUCKETS - max_exact)).astype(jnp.int32)
        large = jnp.minimum(large, N_BUCKETS - 1)
        tabs.append(jnp.where(n < max_exact, n, large))
    return jnp.stack(tabs).astype(jnp.int32)


def _dilated_attention(proj, rel_bias, batch, seq, heads):
    assert seq % DIL_SPAN == 0
    m = proj.shape[1]
    n_pat = len(DILATED_PATTERNS)
    return pl.pallas_call(
        functools.partial(_dilated_kernel, seq=seq),
        out_shape=jax.ShapeDtypeStruct((heads, m, LANES), BF16),
        grid=(batch, heads),
        in_specs=[pl.BlockSpec(memory_space=pltpu.SMEM),
                  pl.BlockSpec((n_pat, BLK, 2 * BLK), lambda b, h: (0, 0, 0)),
                  pl.BlockSpec((1, seq, LANES), lambda b, h: (h, b, 0)),
                  pl.BlockSpec((1, seq, LANES), lambda b, h: (heads + h, b, 0)),
                  pl.BlockSpec((1, seq, LANES), lambda b, h: (2 * heads + h, b, 0))],
        out_specs=pl.BlockSpec((1, seq, LANES), lambda b, h: (h, b, 0)),
        scratch_shapes=[pltpu.VMEM((seq, LANES), F32), pltpu.VMEM((seq, LANES), F32),
                        pltpu.VMEM((seq, LANES), F32),
                        pltpu.VMEM((n_pat, BLK, 2 * BLK), F32),
                        pltpu.VMEM((n_pat, DIL_SPAN, LANES), F32),
                        pltpu.VMEM((n_pat, DIL_SPAN, LANES), F32)],
        compiler_params=_params("arbitrary", "arbitrary"),
        name="dilated_attention",
    )(rel_bias, _dilated_buckets(), proj, proj, proj)


def _flash_kernel(qi_tab, ki_tab, *refs, n_parts, has_forget, tq):
    q_refs = refs[:n_parts]
    k_refs = refs[n_parts:2 * n_parts]
    v_ref = refs[2 * n_parts]
    pos = 2 * n_parts + 1
    if has_forget:
        fq_ref, fk_ref = refs[pos], refs[pos + 1]
        pos += 2
    o_ref, m_scr, l_scr, acc_scr = refs[pos:pos + 4]
    t = pl.program_id(2)
    qi = qi_tab[t]
    ki = ki_tab[t]

    @pl.when(ki == 0)
    def _():
        m_scr[...] = jnp.full(m_scr.shape, NEG_INF, F32)
        l_scr[...] = jnp.zeros(l_scr.shape, F32)
        acc_scr[...] = jnp.zeros(acc_scr.shape, F32)

    def step(diagonal):
        if n_parts == 1:
            q = q_refs[0][0]
            k = k_refs[0][0]
        else:
            q = jnp.concatenate([r[0] for r in q_refs], axis=-1)
            k = jnp.concatenate([r[0] for r in k_refs], axis=-1)
        s = _dot_nt(q, k)
        if has_forget:
            s = s + (fq_ref[0, 0] - fk_ref[0, 0])
        if diagonal:
            row = lax.broadcasted_iota(jnp.int32, s.shape, 0)
            col = lax.broadcasted_iota(jnp.int32, s.shape, 1)
            s = jnp.where(col <= row, s, NEG_INF)
        m_prev = m_scr[...]
        m_new = jnp.maximum(m_prev, jnp.max(s, axis=-1, keepdims=True))
        alpha = jnp.exp(m_prev - m_new)
        p = jnp.exp(s - m_new)
        l_scr[...] = alpha * l_scr[...] + jnp.sum(p, axis=-1, keepdims=True)
        acc_scr[...] = alpha * acc_scr[...] + _dot(p.astype(BF16), v_ref[0])
        m_scr[...] = m_new

    @pl.when(ki < qi)
    def _():
        step(False)

    @pl.when(ki == qi)
    def _():
        step(True)
        o_ref[0] = (acc_scr[...] * (1.0 / l_scr[...])).astype(BF16)


def _flash_attention(q_parts, k_parts, v_part, forget, batch, seq, heads, name):
    tq = min(512, seq)
    nq = seq // tq
    pairs = [(a, b) for a in range(nq) for b in range(a + 1)]
    qi_tab = jnp.asarray([a for a, _ in pairs], jnp.int32)
    ki_tab = jnp.asarray([b for _, b in pairs], jnp.int32)
    m = batch * seq

    def q_spec(fn):
        return pl.BlockSpec((1, tq, LANES), lambda b, h, t, qt, kt: (fn(h), b * nq + qt[t], 0))

    def k_spec(fn):
        return pl.BlockSpec((1, tq, LANES), lambda b, h, t, qt, kt: (fn(h), b * nq + kt[t], 0))

    in_specs = ([q_spec(fn) for _, fn in q_parts] + [k_spec(fn) for _, fn in k_parts]
                + [k_spec(v_part[1])])
    args = [a for a, _ in q_parts] + [a for a, _ in k_parts] + [v_part[0]]
    if forget is not None:
        f_col, f_row = forget
        in_specs.append(pl.BlockSpec((1, 1, tq, 1), lambda b, h, t, qt, kt: (b, h, qt[t], 0)))
        in_specs.append(pl.BlockSpec((1, 1, 1, tq), lambda b, h, t, qt, kt: (b, h, 0, kt[t])))
        args += [f_col, f_row]
    kern = functools.partial(_flash_kernel, n_parts=len(q_parts), has_forget=forget is not None, tq=tq)
    return pl.pallas_call(
        kern,
        out_shape=jax.ShapeDtypeStruct((heads, m, LANES), BF16),
        grid_spec=pltpu.PrefetchScalarGridSpec(
            num_scalar_prefetch=2,
            grid=(batch, heads, len(pairs)),
            in_specs=in_specs,
            out_specs=pl.BlockSpec((1, tq, LANES), lambda b, h, t, qt, kt: (h, b * nq + qt[t], 0)),
            scratch_shapes=[pltpu.VMEM((tq, 1), F32), pltpu.VMEM((tq, 1), F32),
                            pltpu.VMEM((tq, LANES), F32)]),
        compiler_params=_params("arbitrary", "arbitrary", "arbitrary"),
        name=name,
    )(qi_tab, ki_tab, *args)


def _outproj_kernel(*refs, group_counts):
    n_o = len(group_counts)
    o_refs = refs[:n_o]
    z_ref, w_ref, x_ref, gate_ref, out_ref, a_scr = refs[n_o:]

    @pl.when(pl.program_id(1) == 0)
    def _():
        g0 = 0
        for o_ref, cnt in zip(o_refs, group_counts):
            for gi in range(cnt):
                z = z_ref[g0 + gi].astype(F32)
                o = o_ref[gi].astype(F32)
                a_scr[:, (g0 + gi) * LANES:(g0 + gi + 1) * LANES] = (o * _silu(z)).astype(BF16)
            g0 += cnt

    y = _dot(a_scr[...], w_ref[...])
    out_ref[...] = x_ref[...] + gate_ref[0] * y


def _out_projection(o_list, z_arr, z_block, w, x2, gate, seq, name):
    m, d = x2.shape
    width = w.shape[0]
    groups = width // LANES
    tm = min(512, seq)
    tn = min(512, d)
    per_batch = seq // tm
    counts = tuple(o.shape[0] for o in o_list)
    assert sum(counts) == groups
    in_specs = [pl.BlockSpec((cnt, tm, LANES), lambda i, j: (0, i, 0)) for cnt in counts]
    in_specs += [pl.BlockSpec((groups, tm, LANES), lambda i, j: (z_block, i, 0)),
                 pl.BlockSpec((width, tn), lambda i, j: (0, j)),
                 pl.BlockSpec((tm, tn), lambda i, j: (i, j)),
                 pl.BlockSpec((1, 1, tn), lambda i, j: (i // per_batch, 0, j))]
    return pl.pallas_call(
        functools.partial(_outproj_kernel, group_counts=counts),
        out_shape=jax.ShapeDtypeStruct((m, d), F32),
        grid=(m // tm, d // tn),
        in_specs=in_specs,
        out_specs=pl.BlockSpec((tm, tn), lambda i, j: (i, j)),
        scratch_shapes=[pltpu.VMEM((tm, width), BF16)],
        compiler_params=_params("arbitrary", "arbitrary"),
        name=name,
    )(*o_list, z_arr, w, x2, gate[:, None, :])


def _rms_matmul_kernel(c_ref, g_ref, *refs, n_in, rotary, out_scale, tn):
    if rotary:
        wx_ref, wy_ref, cos_ref, sin_ref, o_ref, a_scr = refs
    else:
        wx_ref, o_ref, a_scr = refs

    @pl.when(pl.program_id(1) == 0)
    def _():
        xs = [c_ref[gi].astype(F32) for gi in range(n_in)]
        ss = functools.reduce(lambda a, b: a + b,
                              [jnp.sum(x * x, axis=-1, keepdims=True) for x in xs])
        inv = lax.rsqrt(ss * (1.0 / (n_in * LANES)) + EPS)
        for gi in range(n_in):
            gain = g_ref[:, gi * LANES:(gi + 1) * LANES]
            a_scr[:, gi * LANES:(gi + 1) * LANES] = (xs[gi] * inv * gain).astype(BF16)

    a = a_scr[...]
    acc_x = _dot(a, wx_ref[...])
    if rotary:
        acc_y = _dot(a, wy_ref[...])
        cos = cos_ref[...]
        sin = sin_ref[...]
    for gi in range(tn // LANES):
        sl = slice(gi * LANES, (gi + 1) * LANES)
        if rotary:
            val = acc_x[:, sl] * cos + acc_y[:, sl] * sin
        else:
            val = acc_x[:, sl]
        o_ref[gi] = (val * out_scale).astype(BF16)


def _rms_matmul(src, src_block, n_in, gain, wx, wy, tables, out_scale, seq, name):
    m = src.shape[1]
    kdim, n = wx.shape
    assert kdim == n_in * LANES
    tm = min(1024, seq)
    tn = min(1024, n)
    rotary = wy is not None
    in_specs = [pl.BlockSpec((n_in, tm, LANES), lambda i, j: (src_block, i, 0)),
                pl.BlockSpec((1, kdim), lambda i, j: (0, 0)),
                pl.BlockSpec((kdim, tn), lambda i, j: (0, j))]
    args = [src, gain.reshape(1, kdim), wx]
    if rotary:
        in_specs += [pl.BlockSpec((kdim, tn), lambda i, j: (0, j)),
                     pl.BlockSpec((tm, LANES), lambda i, j: (i, 0)),
                     pl.BlockSpec((tm, LANES), lambda i, j: (i, 0))]
        args += [wy, tables[0], tables[1]]
    kern = functools.partial(_rms_matmul_kernel, n_in=n_in, rotary=rotary, out_scale=out_scale, tn=tn)
    return pl.pallas_call(
        kern,
        out_shape=jax.ShapeDtypeStruct((n // LANES, m, LANES), BF16),
        grid=(m // tm, n // tn),
        in_specs=in_specs,
        out_specs=pl.BlockSpec((tn // LANES, tm, LANES), lambda i, j: (j, i, 0)),
        scratch_shapes=[pltpu.VMEM((tm, kdim), BF16)],
        compiler_params=_params("arbitrary", "arbitrary"),
        name=name,
    )(*args)


def _rope_table_kernel(pos_ref, freq_ref, sign_ref, cos_ref, sin_ref):
    ang = pos_ref[...] * freq_ref[...]
    cos_ref[...] = jnp.cos(ang)
    sin_ref[...] = jnp.sin(ang) * sign_ref[...]


def _rope_tables(positions):
    m = positions.size
    half = QK_ROPE // 2
    inv_freq = ROPE_THETA ** (-jnp.arange(half, dtype=F32) / half)
    freq_row = jnp.tile(inv_freq, LANES // half).reshape(1, LANES)
    sign_row = jnp.tile(jnp.concatenate([-jnp.ones((half,), F32), jnp.ones((half,), F32)]),
                        LANES // QK_ROPE).reshape(1, LANES)
    tm = min(1024, m)
    return pl.pallas_call(
        _rope_table_kernel,
        out_shape=[jax.ShapeDtypeStruct((m, LANES), F32)] * 2,
        grid=(m // tm,),
        in_specs=[pl.BlockSpec((tm, 1), lambda i: (i, 0)),
                  pl.BlockSpec((1, LANES), lambda i: (0, 0)),
                  pl.BlockSpec((1, LANES), lambda i: (0, 0))],
        out_specs=[pl.BlockSpec((tm, LANES), lambda i: (i, 0))] * 2,
        compiler_params=_params("arbitrary"),
        name="rope_tables",
    )(positions.astype(F32).reshape(m, 1), freq_row, sign_row)


def _key_rope_kernel(x_ref, y_ref, cos_ref, sin_ref, o_ref):
    full = x_ref[0].astype(F32) * cos_ref[...] + y_ref[0].astype(F32) * sin_ref[...]
    lane = lax.broadcasted_iota(jnp.int32, full.shape, 1)
    o_ref[0] = jnp.where(lane < QK_ROPE, full, 0.0).astype(BF16)
    o_ref[1] = jnp.where(lane >= QK_ROPE, full, 0.0).astype(BF16)


def _key_rope(proj, x_group, y_group, tables):
    m = proj.shape[1]
    tm = min(1024, m)
    return pl.pallas_call(
        _key_rope_kernel,
        out_shape=jax.ShapeDtypeStruct((2, m, LANES), BF16),
        grid=(m // tm,),
        in_specs=[pl.BlockSpec((1, tm, LANES), lambda i: (x_group, i, 0)),
                  pl.BlockSpec((1, tm, LANES), lambda i: (y_group, i, 0)),
                  pl.BlockSpec((tm, LANES), lambda i: (i, 0)),
                  pl.BlockSpec((tm, LANES), lambda i: (i, 0))],
        out_specs=pl.BlockSpec((2, tm, LANES), lambda i: (0, i, 0)),
        compiler_params=_params("arbitrary"),
        name="key_rope",
    )(proj, proj, tables[0], tables[1])


def _final_norm_kernel(x_ref, g_ref, o_ref):
    x = x_ref[...]
    ms = jnp.mean(x * x, axis=-1, keepdims=True)
    o_ref[...] = x * lax.rsqrt(ms + EPS) * g_ref[...]


def _final_norm(x2, g):
    m, d = x2.shape
    tm = min(256, m)
    return pl.pallas_call(
        _final_norm_kernel,
        out_shape=jax.ShapeDtypeStruct((m, d), F32),
        grid=(m // tm,),
        in_specs=[pl.BlockSpec((tm, d), lambda i: (i, 0)),
                  pl.BlockSpec((1, d), lambda i: (0, 0))],
        out_specs=pl.BlockSpec((tm, d), lambda i: (i, 0)),
        compiler_params=_params("arbitrary"),
        name="final_rmsnorm",
    )(x2, g.reshape(1, d))


def _even_layer(x2, g, shift, scale, gate, rel_bias, w_in, b_f, w_out, batch, seq):
    d = x2.shape[1]
    heads = d // HEAD_DIM
    ha = heads // 2
    hb = heads - ha
    main = 3 * ha * HEAD_DIM + 3 * hb * HEAD_DIM + d
    w_main = w_in[:, :main].astype(BF16)
    w_f = jnp.pad(w_in[:, main:], ((0, 0), (0, LANES - hb))).astype(BF16)
    q_scale = HEAD_DIM ** -0.5
    col_scale = jnp.ones((main,), F32)
    col_scale = col_scale.at[:ha * HEAD_DIM].set(q_scale)
    col_scale = col_scale.at[3 * ha * HEAD_DIM:3 * ha * HEAD_DIM + hb * HEAD_DIM].set(q_scale)
    proj, f_logit = _in_projection(x2, g, scale, shift, w_main, col_scale.reshape(1, main), w_f,
                                   seq, "even_in_projection")
    o_a = _dilated_attention(proj, rel_bias, batch, seq, ha)
    f_cum = _forget_cumsum(f_logit, jnp.pad(b_f, (0, LANES - hb)).reshape(1, LANES), batch, seq)
    f_bhs = f_cum.reshape(batch, seq, LANES)[:, :, :hb].transpose(0, 2, 1)
    forget = (f_bhs[:, :, :, None], f_bhs[:, :, None, :])
    o_b = _flash_attention([(proj, lambda h: 3 * ha + h)], [(proj, lambda h: 3 * ha + hb + h)],
                           (proj, lambda h: 3 * ha + 2 * hb + h), forget, batch, seq, hb,
                           "forgetting_attention")
    z_block = (3 * ha + 3 * hb) // heads
    assert z_block * heads == 3 * ha + 3 * hb
    return _out_projection([o_a, o_b], proj, z_block, w_out.astype(BF16), x2, gate, seq,
                           "even_out_projection")


def _odd_layer(x2, g, shift, scale, gate, tables, w_in, g_q, g_kv, w_qb, w_kvb, w_out, batch, seq):
    d = x2.shape[1]
    heads = d // HEAD_DIM
    half = QK_ROPE // 2
    q_groups = Q_LORA // LANES
    kv_groups = KV_LORA // LANES
    assert heads % q_groups == 0 and heads % 2 == 0
    w_cq = w_in[:, :Q_LORA]
    w_ckv = w_in[:, Q_LORA:Q_LORA + KV_LORA]
    w_kr = w_in[:, Q_LORA + KV_LORA:Q_LORA + KV_LORA + QK_ROPE]
    w_z = w_in[:, Q_LORA + KV_LORA + QK_ROPE:]
    w_kr_x = jnp.tile(w_kr, (1, LANES // QK_ROPE))
    w_kr_y = jnp.tile(jnp.concatenate([w_kr[:, half:], w_kr[:, :half]], axis=1), (1, LANES // QK_ROPE))
    n_used = d + Q_LORA + KV_LORA + 2 * LANES
    tn = 1024
    n_pad = -(-n_used // tn) * tn
    w_cat = jnp.concatenate([w_z, w_cq, w_ckv, w_kr_x, w_kr_y,
                             jnp.zeros((d, n_pad - n_used), F32)], axis=1).astype(BF16)
    (proj,) = _in_projection(x2, g, scale, shift, w_cat, jnp.ones((1, n_pad), F32), None,
                             seq, "odd_in_projection")
    cq_group0 = heads
    ckv_group0 = heads + q_groups
    kr_x_group = heads + q_groups + kv_groups
    k_rot = _key_rope(proj, kr_x_group, kr_x_group + 1, tables)

    q_scale = (QK_NOPE + QK_ROPE) ** -0.5
    w_q3 = w_qb.reshape(Q_LORA, heads, QK_NOPE + QK_ROPE)
    w_qn = w_q3[:, :, :QK_NOPE].reshape(Q_LORA, heads * QK_NOPE).astype(BF16)
    w_qr = w_q3[:, :, QK_NOPE:]
    w_qr_x = w_qr.reshape(Q_LORA, heads * QK_ROPE).astype(BF16)
    w_qr_y = jnp.concatenate([w_qr[:, :, half:], w_qr[:, :, :half]], axis=2
                             ).reshape(Q_LORA, heads * QK_ROPE).astype(BF16)
    q_nope = _rms_matmul(proj, cq_group0 // q_groups, q_groups, g_q, w_qn, None, None, q_scale,
                         seq, "q_nope_projection")
    q_rot = _rms_matmul(proj, cq_group0 // q_groups, q_groups, g_q, w_qr_x, w_qr_y, tables, q_scale,
                        seq, "q_rope_projection")
    assert ckv_group0 % kv_groups == 0
    kv = _rms_matmul(proj, ckv_group0 // kv_groups, kv_groups, g_kv, w_kvb.astype(BF16), None, None,
                     1.0, seq, "kv_projection")
    o = _flash_attention([(q_nope, lambda h: h), (q_rot, lambda h: h // 2)],
                         [(kv, lambda h: 2 * h), (k_rot, lambda h: h % 2)],
                         (kv, lambda h: 2 * h + 1), None, batch, seq, heads, "latent_attention")
    return _out_projection([o], proj, 0, w_out.astype(BF16), x2, gate, seq, "odd_out_projection")


def kernel(x, c, positions, g_norm, w_ada, b_ada, rel_bias, w_in_even, b_forget, w_out_even,
           w_in_odd, g_q_lora, g_kv_lora, w_q_b, w_kv_b, w_out_odd, g_final):
    batch, seq, d = x.shape
    depth = g_norm.shape[0]
    mod = _modulation(c, w_ada, b_ada)
    tables = _rope_tables(positions)
    x2 = x.reshape(batch * seq, d)
    for layer in range(depth):
        shift = mod[layer, :, :d]
        scale = mod[layer, :, d:2 * d]
        gate = mod[layer, :, 2 * d:]
        i = layer // 2
        if layer % 2 == 0:
            x2 = _even_layer(x2, g_norm[layer], shift, scale, gate, rel_bias, w_in_even[i],
                             b_forget[i], w_out_even[i], batch, seq)
        else:
            x2 = _odd_layer(x2, g_norm[layer], shift, scale, gate, tables, w_in_odd[i],
                            g_q_lora[i], g_kv_lora[i], w_q_b[i], w_kv_b[i], w_out_odd[i],
                            batch, seq)
    return _final_norm(x2, g_final).reshape(batch, seq, d)
```
